```python
import math
import jax
import jax.numpy as jnp
from jax import lax
import numpy as np

D_MODEL = 4096
BATCH = 4
SEQ = 2048
DEPTH = 4
DEC_BATCH = 128
DEC_SEQ = 8
PAST_LEN = 8192
PAGE_SIZE = 128

ROPE_THETA = 10000.0
Q_BLOCK = 128
MLA_HEADS = 16
MLA_Q_LORA = 768
MLA_KV_LORA = 256
MLA_NOPE_DIM = 128
MLA_ROPE_DIM = 64
MLA_V_DIM = 128
MLA_SCALE = (MLA_NOPE_DIM + MLA_ROPE_DIM) ** -0.5
RET_HEADS = 4
RET_QK_DIM = 128
RET_V_DIM = 256
RET_CHUNK = 128
DIFF_HEADS = 8
DIFF_KV_HEADS = 1
DIFF_GROUP = DIFF_HEADS // DIFF_KV_HEADS
DIFF_HEAD_DIM = 64
DIFF_V_DIM = 2 * DIFF_HEAD_DIM
DIFF_SCALE = DIFF_HEAD_DIM ** -0.5
MLA_WIDTH = MLA_HEADS * MLA_V_DIM
RET_WIDTH = RET_HEADS * RET_V_DIM
DIFF_WIDTH = DIFF_HEADS * DIFF_V_DIM
MIX_WIDTH = MLA_WIDTH + RET_WIDTH + DIFF_WIDTH
D_FF = 3 * D_MODEL // 2
ALPHA = (2 * DEPTH) ** 0.25
BETA = (8 * DEPTH) ** -0.25
NEG_INF = -1e30
IN_SPLITS = (MLA_Q_LORA, MLA_KV_LORA, MLA_ROPE_DIM, RET_HEADS * RET_QK_DIM, RET_HEADS * RET_QK_DIM, RET_WIDTH, RET_WIDTH, DIFF_HEADS * 2 * DIFF_HEAD_DIM, DIFF_KV_HEADS * 2 * DIFF_HEAD_DIM, DIFF_KV_HEADS * DIFF_V_DIM)
IN_WIDTH = sum(IN_SPLITS)

kernel_name = 'hybrid_mla_retention_diffattn_macaron_deepnorm_step'


def _layernorm(x, g, b, eps=1e-5):
    xf = x.astype(jnp.float32)
    mu = jnp.mean(xf, axis=-1, keepdims=True)
    xc = xf - mu
    var = jnp.mean(xc * xc, axis=-1, keepdims=True)
    return (xc * lax.rsqrt(var + eps) * g.astype(jnp.float32) + b.astype(jnp.float32)).astype(x.dtype)


def _rmsnorm(x, g, eps=1e-6):
    xf = x.astype(jnp.float32)
    ms = jnp.mean(xf * xf, axis=-1, keepdims=True)
    return (xf * lax.rsqrt(ms + eps) * g.astype(jnp.float32)).astype(x.dtype)


def _rope(x, pos):
    half = x.shape[-1] // 2
    inv = ROPE_THETA ** (-jnp.arange(half, dtype=jnp.float32) / half)
    ang = pos.astype(jnp.float32)[:, None] * inv[None, :]
    shape = (1, pos.shape[0]) + (1,) * (x.ndim - 3) + (half,)
    cos = jnp.cos(ang).reshape(shape).astype(x.dtype)
    sin = jnp.sin(ang).reshape(shape).astype(x.dtype)
    x1, x2 = x[..., :half], x[..., half:]
    return jnp.concatenate([x1 * cos - x2 * sin, x2 * cos + x1 * sin], axis=-1)


def _swiglu(x, w_gate, w_up, w_down):
    return (jax.nn.silu(x @ w_gate) * (x @ w_up)) @ w_down


def _causal(q_pos, k_pos):
    return k_pos[None, :] <= q_pos[:, None]


def _masked_softmax(scores, masks):
    s = jnp.concatenate([jnp.where(m, sc, NEG_INF) for sc, m in zip(scores, masks)], axis=-1)
    p = jax.nn.softmax(s, axis=-1)
    cuts = np.cumsum([sc.shape[-1] for sc in scores])[:-1].tolist()
    return jnp.split(p, cuts, axis=-1)


def _query_blocks(fn, q_arrays, q_pos):
    sq = q_pos.shape[0]
    blk = Q_BLOCK if sq % Q_BLOCK == 0 else sq
    nb = sq // blk

    def split(a):
        return jnp.swapaxes(a.reshape((a.shape[0], nb, blk) + a.shape[2:]), 0, 1)

    xs = tuple(split(a) for a in q_arrays) + (q_pos.reshape(nb, blk),)
    out = lax.map(lambda args: fn(*args), xs)
    out = jnp.swapaxes(out, 0, 1)
    return out.reshape((out.shape[0], sq) + out.shape[3:])


def _mla_attend(q_lat, q_rope, q_pos, key_parts):
    scores = [(jnp.einsum('bqhr,bkr->bhqk', q_lat, c) + jnp.einsum('bqhe,bke->bhqk', q_rope, kr)).astype(jnp.float32) * MLA_SCALE for c, kr, _ in key_parts]
    probs = _masked_softmax(scores, [_causal(q_pos, kp) for _, _, kp in key_parts])
    out = None
    for p, (c, _, _) in zip(probs, key_parts):
        o = jnp.einsum('bhqk,bkr->bqhr', p.astype(c.dtype), c)
        out = o if out is None else out + o
    return out


def _diff_attend(q, q_pos, key_parts, lam):
    scores = [jnp.einsum('bqjgmd,bkjmd->bjgmqk', q, k).astype(jnp.float32) * DIFF_SCALE for k, _, _ in key_parts]
    probs = _masked_softmax(scores, [_causal(q_pos, kp) for _, _, kp in key_parts])
    out = None
    for p, (_, v, _) in zip(probs, key_parts):
        wgt = (p[:, :, :, 0] - lam * p[:, :, :, 1]).astype(v.dtype)
        o = jnp.einsum('bjgqk,bkjv->bqjgv', wgt, v)
        out = o if out is None else out + o
    return out


def _ret_log_gamma():
    return jnp.log1p(-(2.0 ** (-5.0 - jnp.arange(RET_HEADS, dtype=jnp.float32))))


def _ret_chunk(state, q, k, v, log_gamma):
    c = q.shape[1]
    idx = jnp.arange(c, dtype=jnp.float32)
    rel = idx[:, None] - idx[None, :]
    intra = jnp.where(rel >= 0, jnp.exp(log_gamma[:, None, None] * jnp.maximum(rel, 0.0)), 0.0).astype(q.dtype)
    o = jnp.einsum('bhij,bjhv->bihv', jnp.einsum('bihd,bjhd->bhij', q, k) * intra, v)
    q_dec = jnp.exp(log_gamma[None, :] * (idx[:, None] + 1.0)).astype(q.dtype)
    o = o + jnp.einsum('bihd,bhdv->bihv', q, state) * q_dec[None, :, :, None]
    k_dec = jnp.exp(log_gamma[None, :] * (c - 1.0 - idx[:, None])).astype(q.dtype)
    chunk_dec = jnp.exp(log_gamma * c).astype(q.dtype)
    new_state = state * chunk_dec[None, :, None, None] + jnp.einsum('bjhd,bjhv->bhdv', k * k_dec[None, :, :, None], v)
    return new_state, o


def _ret_prompt(q, k, v, log_gamma):
    b, s, h, dk = q.shape
    dv = v.shape[-1]
    c = RET_CHUNK if s % RET_CHUNK == 0 else s
    n = s // c

    def chunks(a):
        return jnp.swapaxes(a.reshape((b, n, c) + a.shape[2:]), 0, 1)

    s0 = jnp.zeros((b, h, dk, dv), q.dtype)
    s_final, o = lax.scan(lambda st, xs: _ret_chunk(st, xs[0], xs[1], xs[2], log_gamma), s0, (chunks(q), chunks(k), chunks(v)))
    return s_final, jnp.swapaxes(o, 0, 1).reshape(b, s, h, dv)


def _mixer(h, pos, past, layer, w):
    b, s, _ = h.shape
    proj = h @ w['w_in']
    cq, ckv, kr, rq, rk, rv, rg, dq, dk, dv = jnp.split(proj, np.cumsum(IN_SPLITS)[:-1].tolist(), axis=-1)

    q = (_rmsnorm(cq, w['mla_q_norm']) @ w['mla_w_uq']).reshape(b, s, MLA_HEADS, MLA_NOPE_DIM + MLA_ROPE_DIM)
    q_lat = jnp.einsum('bshn,hnr->bshr', q[..., :MLA_NOPE_DIM], w['mla_w_uk'])
    q_rope = _rope(q[..., MLA_NOPE_DIM:], pos)
    c_kv = _rmsnorm(ckv, w['mla_kv_norm'])
    k_rope = _rope(kr, pos)
    if past is None:
        parts_a = [(c_kv, k_rope, pos)]
    else:
        parts_a = [(past['mla_c'], past['mla_kr'], past['pos']), (c_kv, k_rope, pos)]
    o_lat = _query_blocks(lambda a, e, p: _mla_attend(a, e, p, parts_a), (q_lat, q_rope), pos)
    o_mla = jnp.einsum('bshr,hre->bshe', o_lat, w['mla_w_uv']).reshape(b, s, MLA_WIDTH)

    rq = _rope(rq.reshape(b, s, RET_HEADS, RET_QK_DIM), pos)
    rk = _rope(rk.reshape(b, s, RET_HEADS, RET_QK_DIM), pos) * (RET_QK_DIM ** -0.5)
    rv = rv.reshape(b, s, RET_HEADS, RET_V_DIM)
    log_gamma = _ret_log_gamma()
    if past is None:
        ret_state, o = _ret_prompt(rq, rk, rv, log_gamma)
    else:
        ret_state, o = _ret_chunk(past['ret'], rq, rk, rv, log_gamma)
    o = _layernorm(o, w['ret_gn_g'].reshape(RET_HEADS, RET_V_DIM), w['ret_gn_b'].reshape(RET_HEADS, RET_V_DIM)).reshape(b, s, RET_WIDTH)
    o_ret = jax.nn.silu(rg) * o

    dq = _rope(dq.reshape(b, s, DIFF_KV_HEADS, DIFF_GROUP, 2, DIFF_HEAD_DIM), pos)
    dk = _rope(dk.reshape(b, s, DIFF_KV_HEADS, 2, DIFF_HEAD_DIM), pos)
    dv = dv.reshape(b, s, DIFF_KV_HEADS, DIFF_V_DIM)
    lam_init = 0.8 - 0.6 * math.exp(-0.3 * layer)
    f32 = jnp.float32
    lam = (jnp.exp(jnp.sum(w['diff_lq1'].astype(f32) * w['diff_lk1'].astype(f32)))
           - jnp.exp(jnp.sum(w['diff_lq2'].astype(f32) * w['diff_lk2'].astype(f32))) + lam_init)
    if past is None:
        parts_c = [(dk, dv, pos)]
    else:
        parts_c = [(past['diff_k'], past['diff_v'], past['pos']), (dk, dv, pos)]
    o = _query_blocks(lambda a, p: _diff_attend(a, p, parts_c, lam), (dq,), pos)
    o_diff = (_rmsnorm(o, w['diff_subln_g']) * (1.0 - lam_init)).reshape(b, s, DIFF_WIDTH)

    y = jnp.concatenate([o_mla, o_ret, o_diff], axis=-1) @ w['w_out']
    new = (c_kv, k_rope, dk.reshape(b, s, DIFF_KV_HEADS * 2 * DIFF_HEAD_DIM), dv.reshape(b, s, DIFF_KV_HEADS * DIFF_V_DIM), ret_state)
    return y, new


def _layer(x, pos, past, layer, w):
    x = _layernorm(ALPHA * x + 0.5 * _swiglu(x, w['ffn1_w_gate'], w['ffn1_w_up'], w['ffn1_w_down']), w['ln1_g'], w['ln1_b'])
    m, new = _mixer(x, pos, past, layer, w)
    x = _layernorm(ALPHA * x + m, w['ln2_g'], w['ln2_b'])
    x = _layernorm(ALPHA * x + 0.5 * _swiglu(x, w['ffn2_w_gate'], w['ffn2_w_up'], w['ffn2_w_down']), w['ln3_g'], w['ln3_b'])
    return x, new


def setup_inputs(seed: int = 0) -> dict:
    key = jax.random.key(seed)
    keys = jax.random.split(key, 40)
    counter = [0]

    def nxt():
        k = keys[counter[0]]
        counter[0] += 1
        return k

    def nrm(shape, scale):
        return jax.random.normal(nxt(), shape, jnp.float32) * scale

    def gain(width):
        return 1.0 + nrm((DEPTH, width), 0.02)

    n_pages = PAST_LEN // PAGE_SIZE
    n_pool = (DEC_BATCH * n_pages * 5) // 4
    d = D_MODEL
    return {
        'x_prompt': nrm((BATCH, SEQ, d), 1.0),
        'x_sample': nrm((DEC_BATCH, DEC_SEQ, d), 1.0),
        'cache_mla_latent': nrm((DEPTH, n_pool, PAGE_SIZE, MLA_KV_LORA), 1.0),
        'cache_mla_krope': nrm((DEPTH, n_pool, PAGE_SIZE, MLA_ROPE_DIM), 1.0),
        'cache_diff_k': nrm((DEPTH, n_pool, PAGE_SIZE, DIFF_KV_HEADS * 2 * DIFF_HEAD_DIM), 1.0),
        'cache_diff_v': nrm((DEPTH, n_pool, PAGE_SIZE, DIFF_KV_HEADS * DIFF_V_DIM), 1.0),
        'state_retention': nrm((DEPTH, DEC_BATCH, RET_HEADS, RET_QK_DIM, RET_V_DIM), RET_QK_DIM ** -0.5),
        'page_table': jax.random.permutation(nxt(), n_pool)[:DEC_BATCH * n_pages].reshape(DEC_BATCH, n_pages).astype(jnp.int32),
        'ln1_g': gain(d),
        'ln1_b': nrm((DEPTH, d), 0.02),
        'ffn1_w_gate': nrm((DEPTH, d, D_FF), d ** -0.5),
        'ffn1_w_up': nrm((DEPTH, d, D_FF), d ** -0.5),
        'ffn1_w_down': nrm((DEPTH, D_FF, d), BETA * D_FF ** -0.5),
        'w_in': nrm((DEPTH, d, IN_WIDTH), d ** -0.5),
        'mla_q_norm': gain(MLA_Q_LORA),
        'mla_w_uq': nrm((DEPTH, MLA_Q_LORA, MLA_HEADS * (MLA_NOPE_DIM + MLA_ROPE_DIM)), MLA_Q_LORA ** -0.5),
        'mla_w_uk': nrm((DEPTH, MLA_HEADS, MLA_NOPE_DIM, MLA_KV_LORA), MLA_KV_LORA ** -0.5),
        'mla_kv_norm': gain(MLA_KV_LORA),
        'mla_w_uv': nrm((DEPTH, MLA_HEADS, MLA_KV_LORA, MLA_V_DIM), MLA_KV_LORA ** -0.5),
        'ret_gn_g': gain(RET_WIDTH),
        'ret_gn_b': nrm((DEPTH, RET_WIDTH), 0.02),
        'diff_lq1': nrm((DEPTH, DIFF_HEAD_DIM), 0.1),
        'diff_lk1': nrm((DEPTH, DIFF_HEAD_DIM), 0.1),
        'diff_lq2': nrm((DEPTH, DIFF_HEAD_DIM), 0.1),
        'diff_lk2': nrm((DEPTH, DIFF_HEAD_DIM), 0.1),
        'diff_subln_g': gain(DIFF_V_DIM),
        'w_out': nrm((DEPTH, MIX_WIDTH, d), BETA * MIX_WIDTH ** -0.5),
        'ln2_g': gain(d),
        'ln2_b': nrm((DEPTH, d), 0.02),
        'ffn2_w_gate': nrm((DEPTH, d, D_FF), d ** -0.5),
        'ffn2_w_up': nrm((DEPTH, d, D_FF), d ** -0.5),
        'ffn2_w_down': nrm((DEPTH, D_FF, d), BETA * D_FF ** -0.5),
        'ln3_g': gain(d),
        'ln3_b': nrm((DEPTH, d), 0.02),
    }


def reference(x_prompt, x_sample, cache_mla_latent, cache_mla_krope, cache_diff_k, cache_diff_v, state_retention, page_table,
              ln1_g, ln1_b, ffn1_w_gate, ffn1_w_up, ffn1_w_down, w_in, mla_q_norm, mla_w_uq, mla_w_uk, mla_kv_norm, mla_w_uv,
              ret_gn_g, ret_gn_b, diff_lq1, diff_lk1, diff_lq2, diff_lk2, diff_subln_g, w_out, ln2_g, ln2_b,
              ffn2_w_gate, ffn2_w_up, ffn2_w_down, ln3_g, ln3_b):
    bd, n_pages = page_table.shape
    past_len = n_pages * PAGE_SIZE
    pos_p = jnp.arange(x_prompt.shape[1], dtype=jnp.int32)
    pos_s = past_len + jnp.arange(x_sample.shape[1], dtype=jnp.int32)
    past_pos = jnp.arange(past_len, dtype=jnp.int32)
    xp, xs = x_prompt, x_sample
    new_p, new_s = [], []
    for l in range(DEPTH):
        w = {
            'ln1_g': ln1_g[l], 'ln1_b': ln1_b[l],
            'ffn1_w_gate': ffn1_w_gate[l], 'ffn1_w_up': ffn1_w_up[l], 'ffn1_w_down': ffn1_w_down[l],
            'w_in': w_in[l], 'mla_q_norm': mla_q_norm[l], 'mla_w_uq': mla_w_uq[l], 'mla_w_uk': mla_w_uk[l],
            'mla_kv_norm': mla_kv_norm[l], 'mla_w_uv': mla_w_uv[l],
            'ret_gn_g': ret_gn_g[l], 'ret_gn_b': ret_gn_b[l],
            'diff_lq1': diff_lq1[l], 'diff_lk1': diff_lk1[l], 'diff_lq2': diff_lq2[l], 'diff_lk2': diff_lk2[l],
            'diff_subln_g': diff_subln_g[l], 'w_out': w_out[l],
            'ln2_g': ln2_g[l], 'ln2_b': ln2_b[l],
            'ffn2_w_gate': ffn2_w_gate[l], 'ffn2_w_up': ffn2_w_up[l], 'ffn2_w_down': ffn2_w_down[l],
            'ln3_g': ln3_g[l], 'ln3_b': ln3_b[l],
        }
        xp, newp = _layer(xp, pos_p, None, l, w)
        past = {
            'pos': past_pos,
            'mla_c': cache_mla_latent[l, page_table].reshape(bd, past_len, MLA_KV_LORA),
            'mla_kr': cache_mla_krope[l, page_table].reshape(bd, past_len, MLA_ROPE_DIM),
            'diff_k': cache_diff_k[l, page_table].reshape(bd, past_len, DIFF_KV_HEADS, 2, DIFF_HEAD_DIM),
            'diff_v': cache_diff_v[l, page_table].reshape(bd, past_len, DIFF_KV_HEADS, DIFF_V_DIM),
            'ret': state_retention[l],
        }
        xs, news = _layer(xs, pos_s, past, l, w)
        new_p.append(newp)
        new_s.append(news)

    def stk(lst, i):
        return jnp.stack([n[i] for n in lst])

    return (xp, xs,
            stk(new_p, 0), stk(new_p, 1), stk(new_p, 2), stk(new_p, 3), stk(new_p, 4),
            stk(new_s, 0), stk(new_s, 1), stk(new_s, 2), stk(new_s, 3), stk(new_s, 4))
```

```python
import functools
import math

import jax
import jax.numpy as jnp
from jax import lax
from jax.experimental import pallas as pl
from jax.experimental.pallas import tpu as pltpu

F32 = jnp.float32
BF16 = jnp.bfloat16

D_MODEL = 4096
PAGE = 128
ROPE_THETA = 10000.0
MLA_HEADS = 16
MLA_Q_LORA = 768
MLA_KV_LORA = 256
MLA_NOPE = 128
MLA_ROPE = 64
MLA_V = 128
MLA_SCALE = (MLA_NOPE + MLA_ROPE) ** -0.5
RET_HEADS = 4
RET_QK = 128
RET_V = 256
RET_CHUNK = 128
RET_SCALE = RET_QK ** -0.5
DIFF_HEADS = 8
DIFF_HD = 64
DIFF_V = 128
DIFF_SCALE = DIFF_HD ** -0.5
D_FF = 3 * D_MODEL // 2
NEG_INF = -1e30
LANES = 128

C_CQ = 0
C_CKV = 768
C_RQ = 1024
C_RK = 1536
C_RV = 2048
C_RG = 3072
C_DQ = 4096
C_DK = 5120
C_DV = 5248
C_KR = 5376
PROJ_W = 5632

VMEM_LIMIT = 60 * 1024 * 1024


def _cparams(n_axes):
    return pltpu.CompilerParams(dimension_semantics=("arbitrary",) * n_axes, vmem_limit_bytes=VMEM_LIMIT)


def _swiglu_kernel(x_ref, wg_ref, wu_ref, o_ref, wgb, wub):
    @pl.when(pl.program_id(1) == 0)
    def _():
        wgb[...] = wg_ref[...].astype(BF16)
        wub[...] = wu_ref[...].astype(BF16)

    x = x_ref[...]
    g = jnp.dot(x, wgb[...], preferred_element_type=F32)
    u = jnp.dot(x, wub[...], preferred_element_type=F32)
    o_ref[...] = (g * jax.nn.sigmoid(g) * u).astype(o_ref.dtype)


def _swiglu(xb, wg, wu, bm, bn):
    m, k = xb.shape
    n = wg.shape[1]
    return pl.pallas_call(
        _swiglu_kernel,
        grid=(n // bn, m // bm),
        in_specs=[pl.BlockSpec((bm, k), lambda j, i: (i, 0)),
                  pl.BlockSpec((k, bn), lambda j, i: (0, j)),
                  pl.BlockSpec((k, bn), lambda j, i: (0, j))],
        out_specs=pl.BlockSpec((bm, bn), lambda j, i: (i, j)),
        out_shape=jax.ShapeDtypeStruct((m, n), BF16),
        scratch_shapes=[pltpu.VMEM((k, bn), BF16), pltpu.VMEM((k, bn), BF16)],
        compiler_params=_cparams(2),
        name="swiglu_up",
    )(xb, wg, wu)


def _resid_kernel(x_ref, w_ref, r_ref, o_ref, wb, *, alpha, c):
    @pl.when(pl.program_id(1) == 0)
    def _():
        wb[...] = w_ref[...].astype(BF16)

    y = jnp.dot(x_ref[...], wb[...], preferred_element_type=F32)
    o_ref[...] = alpha * r_ref[...] + c * y


def _mm_resid(xb, w, r, alpha, c, bm, bn):
    m, k = xb.shape
    n = w.shape[1]
    return pl.pallas_call(
        functools.partial(_resid_kernel, alpha=alpha, c=c),
        grid=(n // bn, m // bm),
        in_specs=[pl.BlockSpec((bm, k), lambda j, i: (i, 0)),
                  pl.BlockSpec((k, bn), lambda j, i: (0, j)),
                  pl.BlockSpec((bm, bn), lambda j, i: (i, j))],
        out_specs=pl.BlockSpec((bm, bn), lambda j, i: (i, j)),
        out_shape=jax.ShapeDtypeStruct((m, n), F32),
        scratch_shapes=[pltpu.VMEM((k, bn), BF16)],
        compiler_params=_cparams(2),
        name="mm_resid",
    )(xb, w, r)


def _mm_kernel(x_ref, w_ref, o_ref):
    o_ref[...] = jnp.dot(x_ref[...], w_ref[...], preferred_element_type=F32)


def _mm(xb, wb, bm, bn):
    m, k = xb.shape
    n = wb.shape[1]
    return pl.pallas_call(
        _mm_kernel,
        grid=(n // bn, m // bm),
        in_specs=[pl.BlockSpec((bm, k), lambda j, i: (i, 0)),
                  pl.BlockSpec((k, bn), lambda j, i: (0, j))],
        out_specs=pl.BlockSpec((bm, bn), lambda j, i: (i, j)),
        out_shape=jax.ShapeDtypeStruct((m, n), F32),
        compiler_params=_cparams(2),
        name="mm",
    )(xb, wb)


def _ln_kernel(z_ref, g_ref, b_ref, o_ref, ob_ref):
    z = z_ref[...]
    mu = jnp.mean(z, axis=-1, keepdims=True)
    zc = z - mu
    var = jnp.mean(zc * zc, axis=-1, keepdims=True)
    y = zc * lax.rsqrt(var + 1e-5) * g_ref[...] + b_ref[...]
    o_ref[...] = y
    ob_ref[...] = y.astype(BF16)


def _layernorm(z, g, b, bm):
    m, d = z.shape
    return pl.pallas_call(
        _ln_kernel,
        grid=(m // bm,),
        in_specs=[pl.BlockSpec((bm, d), lambda i: (i, 0)),
                  pl.BlockSpec((1, d), lambda i: (0, 0)),
                  pl.BlockSpec((1, d), lambda i: (0, 0))],
        out_specs=[pl.BlockSpec((bm, d), lambda i: (i, 0)),
                   pl.BlockSpec((bm, d), lambda i: (i, 0))],
        out_shape=[jax.ShapeDtypeStruct((m, d), F32), jax.ShapeDtypeStruct((m, d), BF16)],
        compiler_params=_cparams(1),
        name="layernorm",
    )(z, g.reshape(1, d), b.reshape(1, d))


def _rope64(x, c, s):
    lane = lax.broadcasted_iota(jnp.int32, x.shape, 1)
    rot = jnp.where((lane % 64) < 32, pltpu.roll(x, 96, 1), pltpu.roll(x, 32, 1))
    return x * c + rot * s


def _rope128(x, c, s):
    return x * c + pltpu.roll(x, 64, 1) * s


def _rmsnorm(x, g, eps=1e-6):
    ms = jnp.mean(x * x, axis=-1, keepdims=True)
    return x * lax.rsqrt(ms + eps) * g


def _kvprep_kernel(cq_ref, ckv_ref, dk_ref, dv_ref, kr_ref, c64_ref, s64_ref, gq_ref, gkv_ref,
                   cqn_ref, ckvn_ref, kro_ref, dko_ref, dvo_ref):
    cqn_ref[...] = _rmsnorm(cq_ref[...], gq_ref[...]).astype(BF16)
    ckvn_ref[...] = _rmsnorm(ckv_ref[...], gkv_ref[...])
    c = c64_ref[...]
    s = s64_ref[...]
    kro_ref[...] = _rope64(kr_ref[...], c, s)[:, :MLA_ROPE]
    dko_ref[...] = _rope64(dk_ref[...], c, s)
    dvo_ref[...] = dv_ref[...]


def _kvprep(proj, c64, s64, gq, gkv, bm):
    m = proj.shape[0]

    def col(width, start):
        return pl.BlockSpec((bm, width), lambda i: (i, start // width))

    row = lambda width: pl.BlockSpec((bm, width), lambda i: (i, 0))
    return pl.pallas_call(
        _kvprep_kernel,
        grid=(m // bm,),
        in_specs=[col(MLA_Q_LORA, C_CQ), col(MLA_KV_LORA, C_CKV), col(LANES, C_DK), col(LANES, C_DV),
                  col(LANES, C_KR), row(LANES), row(LANES),
                  pl.BlockSpec((1, MLA_Q_LORA), lambda i: (0, 0)),
                  pl.BlockSpec((1, MLA_KV_LORA), lambda i: (0, 0))],
        out_specs=[row(MLA_Q_LORA), row(MLA_KV_LORA), row(MLA_ROPE), row(2 * DIFF_HD), row(DIFF_V)],
        out_shape=[jax.ShapeDtypeStruct((m, MLA_Q_LORA), BF16),
                   jax.ShapeDtypeStruct((m, MLA_KV_LORA), F32),
                   jax.ShapeDtypeStruct((m, MLA_ROPE), F32),
                   jax.ShapeDtypeStruct((m, 2 * DIFF_HD), F32),
                   jax.ShapeDtypeStruct((m, DIFF_V), F32)],
        compiler_params=_cparams(1),
        name="kvprep",
    )(proj, proj, proj, proj, proj, c64, s64, gq.reshape(1, -1), gkv.reshape(1, -1))


def _qprep_kernel(qn_ref, qr_ref, wuk_ref, c64_ref, s64_ref, qlat_ref, qrope_ref):
    c = c64_ref[...]
    s = s64_ref[...]
    for h in range(MLA_HEADS):
        qn = qn_ref[:, h * MLA_NOPE:(h + 1) * MLA_NOPE].astype(BF16)
        qlat_ref[:, h * MLA_KV_LORA:(h + 1) * MLA_KV_LORA] = jnp.dot(qn, wuk_ref[h], preferred_element_type=F32)
        qrope_ref[:, h * LANES:(h + 1) * LANES] = _rope64(qr_ref[:, h * LANES:(h + 1) * LANES], c, s)


def _qprep(q, wuk_b, c64, s64, bm):
    m = q.shape[0]
    nope_w = MLA_HEADS * MLA_NOPE
    return pl.pallas_call(
        _qprep_kernel,
        grid=(m // bm,),
        in_specs=[pl.BlockSpec((bm, nope_w), lambda i: (i, 0)),
                  pl.BlockSpec((bm, MLA_HEADS * LANES), lambda i: (i, 1)),
                  pl.BlockSpec((MLA_HEADS, MLA_NOPE, MLA_KV_LORA), lambda i: (0, 0, 0)),
                  pl.BlockSpec((bm, LANES), lambda i: (i, 0)),
                  pl.BlockSpec((bm, LANES), lambda i: (i, 0))],
        out_specs=[pl.BlockSpec((bm, MLA_HEADS * MLA_KV_LORA), lambda i: (i, 0)),
                   pl.BlockSpec((bm, MLA_HEADS * LANES), lambda i: (i, 0))],
        out_shape=[jax.ShapeDtypeStruct((m, MLA_HEADS * MLA_KV_LORA), F32),
                   jax.ShapeDtypeStruct((m, MLA_HEADS * LANES), F32)],
        compiler_params=_cparams(1),
        name="qprep",
    )(q, q, wuk_b, c64, s64)


def _uv_kernel(o_ref, wuv_ref, out_ref):
    for h in range(MLA_HEADS):
        o = o_ref[:, h * MLA_KV_LORA:(h + 1) * MLA_KV_LORA].astype(BF16)
        out_ref[:, h * MLA_V:(h + 1) * MLA_V] = jnp.dot(o, wuv_ref[h], preferred_element_type=F32).astype(out_ref.dtype)


def _uv(olat, wuv_b, bm):
    m = olat.shape[0]
    return pl.pallas_call(
        _uv_kernel,
        grid=(m // bm,),
        in_specs=[pl.BlockSpec((bm, MLA_HEADS * MLA_KV_LORA), lambda i: (i, 0)),
                  pl.BlockSpec((MLA_HEADS, MLA_KV_LORA, MLA_V), lambda i: (0, 0, 0))],
        out_specs=pl.BlockSpec((bm, MLA_HEADS * MLA_V), lambda i: (i, 0)),
        out_shape=jax.ShapeDtypeStruct((m, MLA_HEADS * MLA_V), BF16),
        compiler_params=_cparams(1),
        name="mla_uv",
    )(olat, wuv_b)


def _dot_nt(a, b):
    return lax.dot_general(a, b, (((1,), (1,)), ((), ())), preferred_element_type=F32)


def _online_update(s, v, m_ref, l_ref, acc_ref):
    m_old = m_ref[...]
    m_new = jnp.maximum(m_old, jnp.max(s, axis=-1, keepdims=True))
    corr = jnp.exp(m_old - m_new)
    p = jnp.exp(s - m_new)
    l_ref[...] = l_ref[...] * corr + jnp.sum(p, axis=-1, keepdims=True)
    acc_ref[...] = acc_ref[...] * corr + jnp.dot(p.astype(BF16), v, preferred_element_type=F32)
    m_ref[...] = m_new


def _causal_mask(s, q_pos0, k_pos0, tq):
    row = lax.broadcasted_iota(jnp.int32, s.shape, 0)
    col = lax.broadcasted_iota(jnp.int32, s.shape, 1)
    keep = (k_pos0 + col) <= (q_pos0 + (row % tq))
    return jnp.where(keep, s, NEG_INF)


def _mla_prompt_kernel(qlat_ref, qrope_ref, c_ref, kr_ref, o_ref, qs, qrs, krb, m_ref, l_ref, acc_ref, *, tq, tk):
    qi = pl.program_id(1)
    ki = pl.program_id(2)

    @pl.when(ki == 0)
    def _():
        for h in range(MLA_HEADS):
            qs[h * tq:(h + 1) * tq, :] = qlat_ref[:, h * MLA_KV_LORA:(h + 1) * MLA_KV_LORA].astype(BF16)
            qrs[h * tq:(h + 1) * tq, :] = qrope_ref[:, h * LANES:(h + 1) * LANES].astype(BF16)
        krb[...] = jnp.zeros_like(krb)
        m_ref[...] = jnp.full_like(m_ref, NEG_INF)
        l_ref[...] = jnp.zeros_like(l_ref)
        acc_ref[...] = jnp.zeros_like(acc_ref)

    @pl.when(ki <= qi)
    def _():
        c = c_ref[...].astype(BF16)
        krb[:, :MLA_ROPE] = kr_ref[...].astype(BF16)
        s = (_dot_nt(qs[...], c) + _dot_nt(qrs[...], krb[...])) * MLA_SCALE
        s = _causal_mask(s, qi * tq, ki * tk, tq)
        _online_update(s, c, m_ref, l_ref, acc_ref)

    @pl.when(ki == qi)
    def _():
        o = acc_ref[...] / l_ref[...]
        for h in range(MLA_HEADS):
            o_ref[:, h * MLA_KV_LORA:(h + 1) * MLA_KV_LORA] = o[h * tq:(h + 1) * tq, :]


def _mla_prompt(qlat, qrope, ckv, kr, batch, seq, t):
    nq = seq // t
    rows = MLA_HEADS * t
    return pl.pallas_call(
        functools.partial(_mla_prompt_kernel, tq=t, tk=t),
        grid=(batch, nq, nq),
        in_specs=[pl.BlockSpec((t, MLA_HEADS * MLA_KV_LORA), lambda b, i, j: (b * nq + i, 0)),
                  pl.BlockSpec((t, MLA_HEADS * LANES), lambda b, i, j: (b * nq + i, 0)),
                  pl.BlockSpec((t, MLA_KV_LORA), lambda b, i, j: (b * nq + jnp.minimum(i, j), 0)),
                  pl.BlockSpec((t, MLA_ROPE), lambda b, i, j: (b * nq + jnp.minimum(i, j), 0))],
        out_specs=pl.BlockSpec((t, MLA_HEADS * MLA_KV_LORA), lambda b, i, j: (b * nq + i, 0)),
        out_shape=jax.ShapeDtypeStruct((batch * seq, MLA_HEADS * MLA_KV_LORA), F32),
        scratch_shapes=[pltpu.VMEM((rows, MLA_KV_LORA), BF16), pltpu.VMEM((rows, LANES), BF16),
                        pltpu.VMEM((t, LANES), BF16),
                        pltpu.VMEM((rows, 1), F32), pltpu.VMEM((rows, 1), F32),
                        pltpu.VMEM((rows, MLA_KV_LORA), F32)],
        compiler_params=_cparams(3),
        name="mla_prompt",
    )(qlat, qrope, ckv, kr)


def _mla_sample_kernel(pt_ref, qlat_ref, qrope_ref, cn_ref, krn_ref, *rest, pps, dec):
    c_pages = rest[:pps]
    kr_pages = rest[pps:2 * pps]
    o_ref, qs, qrs, cb, krb, cnf, krnf, m_ref, l_ref, acc_ref = rest[2 * pps:]
    j = pl.program_id(1)
    nj = pl.num_programs(1)
    first = jnp.logical_and(pl.program_id(0) == 0, j == 0)

    @pl.when(first)
    def _():
        krb[...] = jnp.zeros_like(krb)
        cnf[...] = jnp.zeros_like(cnf)
        krnf[...] = jnp.zeros_like(krnf)

    @pl.when(j == 0)
    def _():
        for h in range(MLA_HEADS):
            qs[h * dec:(h + 1) * dec, :] = qlat_ref[:, h * MLA_KV_LORA:(h + 1) * MLA_KV_LORA]
            qrs[h * dec:(h + 1) * dec, :] = qrope_ref[:, h * LANES:(h + 1) * LANES]
        m_ref[...] = jnp.full_like(m_ref, NEG_INF)
        l_ref[...] = jnp.zeros_like(l_ref)
        acc_ref[...] = jnp.zeros_like(acc_ref)

    for p in range(pps):
        cb[p * PAGE:(p + 1) * PAGE, :] = c_pages[p][...].astype(BF16)
        krb[p * PAGE:(p + 1) * PAGE, :MLA_ROPE] = kr_pages[p][...].astype(BF16)
    q = qs[...].astype(BF16)
    qr = qrs[...].astype(BF16)
    c = cb[...]
    s = (_dot_nt(q, c) + _dot_nt(qr, krb[...])) * MLA_SCALE
    _online_update(s, c, m_ref, l_ref, acc_ref)

    @pl.when(j == nj - 1)
    def _():
        cnf[:dec, :] = cn_ref[...]
        krnf[:dec, :MLA_ROPE] = krn_ref[...]
        cn = cnf[...].astype(BF16)
        sn = (_dot_nt(q, cn) + _dot_nt(qr, krnf[...].astype(BF16))) * MLA_SCALE
        sn = _causal_mask(sn, 0, 0, dec)
        _online_update(sn, cn, m_ref, l_ref, acc_ref)
        o = acc_ref[...] / l_ref[...]
        for h in range(MLA_HEADS):
            o_ref[:, h * MLA_KV_LORA:(h + 1) * MLA_KV_LORA] = o[h * dec:(h + 1) * dec, :]


def _mla_sample(page_table, qlat, qrope, ckv, kr, cache_c, cache_kr, layer, row0, dec, pps):
    nb, n_pages = page_table.shape
    nj = n_pages // pps
    blk0 = row0 // dec
    rows = MLA_HEADS * dec
    pt = page_table.reshape(-1)

    def page_spec(width, p):
        return pl.BlockSpec((None, None, PAGE, width),
                            lambda b, j, pt_ref: (layer, pt_ref[b * n_pages + j * pps + p], 0, 0))

    tok = lambda width: pl.BlockSpec((dec, width), lambda b, j, pt_ref: (blk0 + b, 0))
    grid_spec = pltpu.PrefetchScalarGridSpec(
        num_scalar_prefetch=1,
        grid=(nb, nj),
        in_specs=[tok(MLA_HEADS * MLA_KV_LORA), tok(MLA_HEADS * LANES), tok(MLA_KV_LORA), tok(MLA_ROPE)]
        + [page_spec(MLA_KV_LORA, p) for p in range(pps)]
        + [page_spec(MLA_ROPE, p) for p in range(pps)],
        out_specs=pl.BlockSpec((dec, MLA_HEADS * MLA_KV_LORA), lambda b, j, pt_ref: (b, 0)),
        scratch_shapes=[pltpu.VMEM((rows, MLA_KV_LORA), F32), pltpu.VMEM((rows, LANES), F32),
                        pltpu.VMEM((pps * PAGE, MLA_KV_LORA), BF16), pltpu.VMEM((pps * PAGE, LANES), BF16),
                        pltpu.VMEM((PAGE, MLA_KV_LORA), F32), pltpu.VMEM((PAGE, LANES), F32),
                        pltpu.VMEM((rows, 1), F32), pltpu.VMEM((rows, 1), F32),
                        pltpu.VMEM((rows, MLA_KV_LORA), F32)],
    )
    return pl.pallas_call(
        functools.partial(_mla_sample_kernel, pps=pps, dec=dec),
        grid_spec=grid_spec,
        out_shape=jax.ShapeDtypeStruct((nb * dec, MLA_HEADS * MLA_KV_LORA), F32),
        compiler_params=_cparams(2),
        name="mla_sample",
    )(pt, qlat, qrope, ckv, kr, *([cache_c] * pps), *([cache_kr] * pps))


def _diff_lambda(lq1, lk1, lq2, lk2, lam_init):
    a = jnp.sum(lq1 * lk1, axis=-1, keepdims=True)
    b = jnp.sum(lq2 * lk2, axis=-1, keepdims=True)
    return jnp.exp(a) - jnp.exp(b) + lam_init


def _diff_stack_q(dq_ref, c, s, qs, t):
    lane = lax.broadcasted_iota(jnp.int32, (t, LANES), 1)
    lo = lane < DIFF_HD
    for g in range(DIFF_HEADS):
        x = _rope64(dq_ref[:, g * LANES:(g + 1) * LANES], c, s)
        qs[g * t:(g + 1) * t, :] = jnp.where(lo, x, 0.0).astype(qs.dtype)
        qs[(DIFF_HEADS + g) * t:(DIFF_HEADS + g + 1) * t, :] = jnp.where(lo, 0.0, x).astype(qs.dtype)


def _diff_finish(o_ref, m_ref, l_ref, acc_ref, lam, gain, lam_init, t):
    half = DIFF_HEADS * t
    on = acc_ref[...] / l_ref[...]
    o = on[:half, :] - lam * on[half:, :]
    o = _rmsnorm(o, gain) * (1.0 - lam_init)
    for g in range(DIFF_HEADS):
        o_ref[:, g * DIFF_V:(g + 1) * DIFF_V] = o[g * t:(g + 1) * t, :].astype(o_ref.dtype)


def _diff_prompt_kernel(dq_ref, c64_ref, s64_ref, dk_ref, dv_ref, lq1, lk1, lq2, lk2, gain_ref, o_ref,
                        qs, m_ref, l_ref, acc_ref, *, t, lam_init):
    qi = pl.program_id(1)
    ki = pl.program_id(2)

    @pl.when(ki == 0)
    def _():
        _diff_stack_q(dq_ref, c64_ref[...], s64_ref[...], qs, t)
        m_ref[...] = jnp.full_like(m_ref, NEG_INF)
        l_ref[...] = jnp.zeros_like(l_ref)
        acc_ref[...] = jnp.zeros_like(acc_ref)

    @pl.when(ki <= qi)
    def _():
        s = _dot_nt(qs[...], dk_ref[...].astype(BF16)) * DIFF_SCALE
        s = _causal_mask(s, qi * t, ki * t, t)
        _online_update(s, dv_ref[...].astype(BF16), m_ref, l_ref, acc_ref)

    @pl.when(ki == qi)
    def _():
        lam = _diff_lambda(lq1[...], lk1[...], lq2[...], lk2[...], lam_init)
        _diff_finish(o_ref, m_ref, l_ref, acc_ref, lam, gain_ref[...], lam_init, t)


def _diff_prompt(proj, c64, s64, dk, dv, lams, gain, lam_init, batch, seq, t):
    nq = seq // t
    rows = 2 * DIFF_HEADS * t
    qw = DIFF_HEADS * LANES
    small = lambda w: pl.BlockSpec((1, w), lambda b, i, j: (0, 0))
    return pl.pallas_call(
        functools.partial(_diff_prompt_kernel, t=t, lam_init=lam_init),
        grid=(batch, nq, nq),
        in_specs=[pl.BlockSpec((t, qw), lambda b, i, j: (b * nq + i, C_DQ // qw)),
                  pl.BlockSpec((t, LANES), lambda b, i, j: (b * nq + i, 0)),
                  pl.BlockSpec((t, LANES), lambda b, i, j: (b * nq + i, 0)),
                  pl.BlockSpec((t, LANES), lambda b, i, j: (b * nq + jnp.minimum(i, j), 0)),
                  pl.BlockSpec((t, LANES), lambda b, i, j: (b * nq + jnp.minimum(i, j), 0)),
                  small(DIFF_HD), small(DIFF_HD), small(DIFF_HD), small(DIFF_HD), small(DIFF_V)],
        out_specs=pl.BlockSpec((t, qw), lambda b, i, j: (b * nq + i, 0)),
        out_shape=jax.ShapeDtypeStruct((batch * seq, qw), F32),
        scratch_shapes=[pltpu.VMEM((rows, LANES), BF16), pltpu.VMEM((rows, 1), F32), pltpu.VMEM((rows, 1), F32),
                        pltpu.VMEM((rows, DIFF_V), F32)],
        compiler_params=_cparams(3),
        name="diff_prompt",
    )(proj, c64, s64, dk, dv, *lams, gain)


def _diff_sample_kernel(pt_ref, dq_ref, c64_ref, s64_ref, dkn_ref, dvn_ref, lq1, lk1, lq2, lk2, gain_ref, *rest,
                        pps, dec, lam_init):
    k_pages = rest[:pps]
    v_pages = rest[pps:2 * pps]
    o_ref, qs, kb, vb, knf, vnf, m_ref, l_ref, acc_ref = rest[2 * pps:]
    j = pl.program_id(1)
    nj = pl.num_programs(1)
    first = jnp.logical_and(pl.program_id(0) == 0, j == 0)

    @pl.when(first)
    def _():
        knf[...] = jnp.zeros_like(knf)
        vnf[...] = jnp.zeros_like(vnf)

    @pl.when(j == 0)
    def _():
        _diff_stack_q(dq_ref, c64_ref[...], s64_ref[...], qs, dec)
        m_ref[...] = jnp.full_like(m_ref, NEG_INF)
        l_ref[...] = jnp.zeros_like(l_ref)
        acc_ref[...] = jnp.zeros_like(acc_ref)

    for p in range(pps):
        kb[p * PAGE:(p + 1) * PAGE, :] = k_pages[p][...].astype(BF16)
        vb[p * PAGE:(p + 1) * PAGE, :] = v_pages[p][...].astype(BF16)
    q = qs[...].astype(BF16)
    s = _dot_nt(q, kb[...]) * DIFF_SCALE
    _online_update(s, vb[...], m_ref, l_ref, acc_ref)

    @pl.when(j == nj - 1)
    def _():
        knf[:dec, :] = dkn_ref[...]
        vnf[:dec, :] = dvn_ref[...]
        sn = _dot_nt(q, knf[...].astype(BF16)) * DIFF_SCALE
        sn = _causal_mask(sn, 0, 0, dec)
        _online_update(sn, vnf[...].astype(BF16), m_ref, l_ref, acc_ref)
        lam = _diff_lambda(lq1[...], lk1[...], lq2[...], lk2[...], lam_init)
        _diff_finish(o_ref, m_ref, l_ref, acc_ref, lam, gain_ref[...], lam_init, dec)


def _diff_sample(page_table, proj, c64, s64, dk, dv, lams, gain, lam_init, cache_k, cache_v, layer, row0, dec, pps):
    nb, n_pages = page_table.shape
    nj = n_pages // pps
    blk0 = row0 // dec
    rows = 2 * DIFF_HEADS * dec
    qw = DIFF_HEADS * LANES
    pt = page_table.reshape(-1)

    def page_spec(p):
        return pl.BlockSpec((None, None, PAGE, LANES),
                            lambda b, j, pt_ref: (layer, pt_ref[b * n_pages + j * pps + p], 0, 0))

    tok = lambda width, cb=0: pl.BlockSpec((dec, width), lambda b, j, pt_ref: (blk0 + b, cb))
    small = lambda w: pl.BlockSpec((1, w), lambda b, j, pt_ref: (0, 0))
    grid_spec = pltpu.PrefetchScalarGridSpec(
        num_scalar_prefetch=1,
        grid=(nb, nj),
        in_specs=[tok(qw, C_DQ // qw), tok(LANES), tok(LANES), tok(LANES), tok(LANES),
                  small(DIFF_HD), small(DIFF_HD), small(DIFF_HD), small(DIFF_HD), small(DIFF_V)]
        + [page_spec(p) for p in range(pps)] + [page_spec(p) for p in range(pps)],
        out_specs=pl.BlockSpec((dec, qw), lambda b, j, pt_ref: (b, 0)),
        scratch_shapes=[pltpu.VMEM((rows, LANES), F32),
                        pltpu.VMEM((pps * PAGE, LANES), BF16), pltpu.VMEM((pps * PAGE, LANES), BF16),
                        pltpu.VMEM((PAGE, LANES), F32), pltpu.VMEM((PAGE, LANES), F32),
                        pltpu.VMEM((rows, 1), F32), pltpu.VMEM((rows, 1), F32), pltpu.VMEM((rows, DIFF_V), F32)],
    )
    return pl.pallas_call(
        functools.partial(_diff_sample_kernel, pps=pps, dec=dec, lam_init=lam_init),
        grid_spec=grid_spec,
        out_shape=jax.ShapeDtypeStruct((nb * dec, qw), F32),
        compiler_params=_cparams(2),
        name="diff_sample",
    )(pt, proj, c64, s64, dk, dv, *lams, gain, *([cache_k] * pps), *([cache_v] * pps))


def _ret_head(q, k, v, state, intra, qdec, kdec, cdec):
    qb = q.astype(BF16)
    vb = v.astype(BF16)
    a = _dot_nt(qb, k.astype(BF16)) * intra
    o = jnp.dot(a.astype(BF16), vb, preferred_element_type=F32)
    o = o + jnp.dot(qb, state.astype(BF16), preferred_element_type=F32) * qdec
    kd = (k * kdec).T.astype(BF16)
    new_state = state * cdec + jnp.dot(kd, vb, preferred_element_type=F32)
    return o, new_state


def _ret_out(o, gate, g, b):
    mu = jnp.mean(o, axis=-1, keepdims=True)
    oc = o - mu
    var = jnp.mean(oc * oc, axis=-1, keepdims=True)
    y = oc * lax.rsqrt(var + 1e-5) * g + b
    return gate * jax.nn.sigmoid(gate) * y


def _ret_prompt_kernel(rq_ref, rk_ref, rv_ref, rg_ref, c128_ref, s128_ref, intra_ref, qdec_ref, kdec_ref, cdec_ref,
                       g_ref, b_ref, o_ref, st_ref, state):
    ci = pl.program_id(1)

    @pl.when(ci == 0)
    def _():
        state[...] = jnp.zeros_like(state)

    c = c128_ref[...]
    s = s128_ref[...]
    for h in range(RET_HEADS):
        q = _rope128(rq_ref[:, h * RET_QK:(h + 1) * RET_QK], c, s)
        k = _rope128(rk_ref[:, h * RET_QK:(h + 1) * RET_QK], c, s) * RET_SCALE
        vs = slice(h * RET_V, (h + 1) * RET_V)
        o, new_state = _ret_head(q, k, rv_ref[:, vs], state[h], intra_ref[h], qdec_ref[h], kdec_ref[h], cdec_ref[h])
        state[h] = new_state
        o_ref[:, vs] = _ret_out(o, rg_ref[:, vs], g_ref[:, vs], b_ref[:, vs])

    @pl.when(ci == pl.num_programs(1) - 1)
    def _():
        st_ref[...] = state[...]


def _ret_tables(c):
    lg = jnp.log1p(-(2.0 ** (-5.0 - jnp.arange(RET_HEADS, dtype=F32))))
    idx = jnp.arange(c, dtype=F32)
    rel = idx[:, None] - idx[None, :]
    intra = jnp.where(rel >= 0, jnp.exp(lg[:, None, None] * jnp.maximum(rel, 0.0)), 0.0).astype(F32)
    qdec = jnp.exp(lg[:, None] * (idx[None, :] + 1.0))
    kdec = jnp.exp(lg[:, None] * (c - 1.0 - idx[None, :]))
    cdec = jnp.exp(lg * c)
    qdec = jnp.broadcast_to(qdec[:, :, None], (RET_HEADS, c, RET_V))
    kdec = jnp.broadcast_to(kdec[:, :, None], (RET_HEADS, c, RET_QK))
    cdec = jnp.broadcast_to(cdec[:, None, None], (RET_HEADS, 1, RET_V))
    return intra, qdec, kdec, cdec


def _ret_prompt(proj, c128, s128, g, b, batch, seq):
    c = RET_CHUNK
    nc = seq // c
    intra, qdec, kdec, cdec = _ret_tables(c)
    qk_w = RET_HEADS * RET_QK
    v_w = RET_HEADS * RET_V
    tok = lambda width, start: pl.BlockSpec((c, width), lambda bi, ci: (bi * nc + ci, start // width))
    full = lambda shape: pl.BlockSpec(shape, lambda bi, ci: (0,) * len(shape))
    return pl.pallas_call(
        _ret_prompt_kernel,
        grid=(batch, nc),
        in_specs=[tok(qk_w, C_RQ), tok(qk_w, C_RK), tok(v_w, C_RV), tok(v_w, C_RG), tok(LANES, 0), tok(LANES, 0),
                  full(intra.shape), full(qdec.shape), full(kdec.shape), full(cdec.shape),
                  full((1, v_w)), full((1, v_w))],
        out_specs=[pl.BlockSpec((c, v_w), lambda bi, ci: (bi * nc + ci, 0)),
                   pl.BlockSpec((None, RET_HEADS, RET_QK, RET_V), lambda bi, ci: (bi, 0, 0, 0))],
        out_shape=[jax.ShapeDtypeStruct((batch * seq, v_w), F32),
                   jax.ShapeDtypeStruct((batch, RET_HEADS, RET_QK, RET_V), F32)],
        scratch_shapes=[pltpu.VMEM((RET_HEADS, RET_QK, RET_V), F32)],
        compiler_params=_cparams(2),
        name="ret_prompt",
    )(proj, proj, proj, proj, c128, s128, intra, qdec, kdec, cdec, g.reshape(1, -1), b.reshape(1, -1))


def _ret_sample_kernel(rq_ref, rk_ref, rv_ref, rg_ref, c128_ref, s128_ref, st_in_ref, intra_ref, qdec_ref, kdec_ref,
                       cdec_ref, g_ref, b_ref, o_ref, st_out_ref, kpad, vpad, *, bb, dec):
    @pl.when(pl.program_id(0) == 0)
    def _():
        kpad[...] = jnp.zeros_like(kpad)
        vpad[...] = jnp.zeros_like(vpad)

    c = c128_ref[...]
    s = s128_ref[...]
    for bi in range(bb):
        rows = slice(bi * dec, (bi + 1) * dec)
        for h in range(RET_HEADS):
            q = _rope128(rq_ref[rows, h * RET_QK:(h + 1) * RET_QK], c[rows], s[rows])
            k = _rope128(rk_ref[rows, h * RET_QK:(h + 1) * RET_QK], c[rows], s[rows]) * RET_SCALE
            vs = slice(h * RET_V, (h + 1) * RET_V)
            kpad[:dec, :] = k
            vpad[:dec, :] = rv_ref[rows, vs]
            o, new_state = _ret_head(q, kpad[...], vpad[...], st_in_ref[bi, h], intra_ref[h], qdec_ref[h],
                                     kdec_ref[h], cdec_ref[h])
            st_out_ref[bi, h] = new_state
            o_ref[rows, vs] = _ret_out(o, rg_ref[rows, vs], g_ref[:, vs], b_ref[:, vs])


def _ret_sample(proj, c128, s128, state_in, g, b, row0, dec, bb):
    nb = state_in.shape[0]
    pad = PAGE
    intra, qdec, kdec, cdec = _ret_tables(dec)
    intra = jnp.pad(intra, ((0, 0), (0, 0), (0, pad - dec)))
    kdec = jnp.pad(kdec, ((0, 0), (0, pad - dec), (0, 0)))
    qk_w = RET_HEADS * RET_QK
    v_w = RET_HEADS * RET_V
    blk0 = row0 // (bb * dec)
    tok = lambda width, start: pl.BlockSpec((bb * dec, width), lambda i: (blk0 + i, start // width))
    full = lambda shape: pl.BlockSpec(shape, lambda i: (0,) * len(shape))
    st_spec = pl.BlockSpec((bb, RET_HEADS, RET_QK, RET_V), lambda i: (i, 0, 0, 0))
    return pl.pallas_call(
        functools.partial(_ret_sample_kernel, bb=bb, dec=dec),
        grid=(nb // bb,),
        in_specs=[tok(qk_w, C_RQ), tok(qk_w, C_RK), tok(v_w, C_RV), tok(v_w, C_RG), tok(LANES, 0), tok(LANES, 0),
                  st_spec, full(intra.shape), full(qdec.shape), full(kdec.shape), full(cdec.shape),
                  full((1, v_w)), full((1, v_w))],
        out_specs=[pl.BlockSpec((bb * dec, v_w), lambda i: (i, 0)), st_spec],
        out_shape=[jax.ShapeDtypeStruct((nb * dec, v_w), F32),
                   jax.ShapeDtypeStruct(state_in.shape, F32)],
        scratch_shapes=[pltpu.VMEM((pad, RET_QK), F32), pltpu.VMEM((pad, RET_V), F32)],
        compiler_params=_cparams(1),
        name="ret_sample",
    )(proj, proj, proj, proj, c128, s128, state_in, intra, qdec, kdec, cdec, g.reshape(1, -1), b.reshape(1, -1))


def _rope_tables(pos):
    p = pos.astype(F32)[:, None]

    def tables(half, reps):
        inv = ROPE_THETA ** (-jnp.arange(half, dtype=F32) / half)
        ang = p * inv[None, :]
        cos, sin = jnp.cos(ang), jnp.sin(ang)
        return jnp.tile(jnp.concatenate([cos, cos], -1), (1, reps)), jnp.tile(jnp.concatenate([-sin, sin], -1), (1, reps))

    c64, s64 = tables(32, 2)
    c128, s128 = tables(64, 1)
    return c64, s64, c128, s128


def _repack_w_in(w):
    cq, ckv, kr, rq, rk, rv, rg, dq, dk, dv = jnp.split(
        w, [768, 1024, 1088, 1600, 2112, 3136, 4160, 5184, 5312], axis=-1)
    pad = jnp.zeros((w.shape[0], PROJ_W - C_KR - MLA_ROPE), w.dtype)
    return jnp.concatenate([cq, ckv, rq, rk, rv, rg, dq, dk, dv, kr, pad], axis=-1).astype(BF16)


def _repack_w_uq(w):
    k = w.shape[0]
    w = w.reshape(k, MLA_HEADS, MLA_NOPE + MLA_ROPE)
    nope = w[:, :, :MLA_NOPE].reshape(k, MLA_HEADS * MLA_NOPE)
    rope = jnp.pad(w[:, :, MLA_NOPE:], ((0, 0), (0, 0), (0, LANES - MLA_ROPE))).reshape(k, MLA_HEADS * LANES)
    return jnp.concatenate([nope, rope], axis=-1).astype(BF16)


def kernel(x_prompt, x_sample, cache_mla_latent, cache_mla_krope, cache_diff_k, cache_diff_v, state_retention, page_table,
           ln1_g, ln1_b, ffn1_w_gate, ffn1_w_up, ffn1_w_down, w_in, mla_q_norm, mla_w_uq, mla_w_uk, mla_kv_norm, mla_w_uv,
           ret_gn_g, ret_gn_b, diff_lq1, diff_lk1, diff_lq2, diff_lk2, diff_subln_g, w_out, ln2_g, ln2_b,
           ffn2_w_gate, ffn2_w_up, ffn2_w_down, ln3_g, ln3_b):
    batch, seq, d = x_prompt.shape
    nb, dec, _ = x_sample.shape
    depth = w_in.shape[0]
    n_pages = page_table.shape[1]
    past_len = n_pages * PAGE
    mp = batch * seq
    ms = nb * dec
    m = mp + ms
    alpha = (2 * depth) ** 0.25

    bm = 1024 if m % 1024 == 0 else 256
    bp = 512 if m % 512 == 0 else 128
    t_attn = 256 if seq % 256 == 0 else 128
    pps = 16 if n_pages % 16 == 0 else n_pages
    bb = 4 if nb % 4 == 0 else 1

    pos = jnp.concatenate([jnp.tile(jnp.arange(seq, dtype=jnp.int32), batch),
                           jnp.tile(past_len + jnp.arange(dec, dtype=jnp.int32), nb)])
    c64, s64, c128, s128 = _rope_tables(pos)

    x = jnp.concatenate([x_prompt.reshape(mp, d), x_sample.reshape(ms, d)], axis=0)
    xb = x.astype(BF16)
    new = []
    for l in range(depth):
        lam_init = 0.8 - 0.6 * math.exp(-0.3 * l)
        h = _swiglu(xb, ffn1_w_gate[l], ffn1_w_up[l], bm, 256)
        z = _mm_resid(h, ffn1_w_down[l], x, alpha, 0.5, bm, 256)
        x, xb = _layernorm(z, ln1_g[l], ln1_b[l], 256)
        proj = _mm(xb, _repack_w_in(w_in[l]), bm, 512)
        cqn, ckv, kr, dk, dv = _kvprep(proj, c64, s64, mla_q_norm[l], mla_kv_norm[l], bp)
        q = _mm(cqn, _repack_w_uq(mla_w_uq[l]), bm, 512)
        qlat, qrope = _qprep(q, mla_w_uk[l].astype(BF16), c64, s64, bp)
        olat_p = _mla_prompt(qlat, qrope, ckv, kr, batch, seq, t_attn)
        olat_s = _mla_sample(page_table, qlat, qrope, ckv, kr, cache_mla_latent, cache_mla_krope, l, mp, dec, pps)
        o_mla = _uv(jnp.concatenate([olat_p, olat_s], axis=0), mla_w_uv[l].astype(BF16), bp)
        oret_p, st_p = _ret_prompt(proj, c128, s128, ret_gn_g[l], ret_gn_b[l], batch, seq)
        oret_s, st_s = _ret_sample(proj, c128, s128, state_retention[l], ret_gn_g[l], ret_gn_b[l], mp, dec, bb)
        lams = [a[l].reshape(1, -1) for a in (diff_lq1, diff_lk1, diff_lq2, diff_lk2)]
        gain = diff_subln_g[l].reshape(1, -1)
        odiff_p = _diff_prompt(proj, c64, s64, dk, dv, lams, gain, lam_init, batch, seq, t_attn)
        odiff_s = _diff_sample(page_table, proj, c64, s64, dk, dv, lams, gain, lam_init, cache_diff_k, cache_diff_v,
                               l, mp, dec, pps)
        mix = jnp.concatenate([o_mla,
                               jnp.concatenate([oret_p, oret_s], axis=0).astype(BF16),
                               jnp.concatenate([odiff_p, odiff_s], axis=0).astype(BF16)], axis=-1)
        z = _mm_resid(mix, w_out[l], x, alpha, 1.0, bm, 256)
        x, xb = _layernorm(z, ln2_g[l], ln2_b[l], 256)
        h = _swiglu(xb, ffn2_w_gate[l], ffn2_w_up[l], bm, 256)
        z = _mm_resid(h, ffn2_w_down[l], x, alpha, 0.5, bm, 256)
        x, xb = _layernorm(z, ln3_g[l], ln3_b[l], 256)
        new.append((ckv, kr, dk, dv, st_p, st_s))

    def stk(i, lo, hi, shape):
        return jnp.stack([n[i][lo:hi].reshape(shape) for n in new])

    y_prompt = x[:mp].reshape(batch, seq, d)
    y_sample = x[mp:].reshape(nb, dec, d)
    return (y_prompt, y_sample,
            stk(0, 0, mp, (batch, seq, MLA_KV_LORA)), stk(1, 0, mp, (batch, seq, MLA_ROPE)),
            stk(2, 0, mp, (batch, seq, 2 * DIFF_HD)), stk(3, 0, mp, (batch, seq, DIFF_V)),
            jnp.stack([n[4] for n in new]),
            stk(0, mp, m, (nb, dec, MLA_KV_LORA)), stk(1, mp, m, (nb, dec, MLA_ROPE)),
            stk(2, mp, m, (nb, dec, 2 * DIFF_HD)), stk(3, mp, m, (nb, dec, DIFF_V)),
            jnp.stack([n[5] for n in new]))
```
